```python
import math
import jax, jax.numpy as jnp
from jax import lax
import numpy as np

D_MODEL = 1024
BATCH = 32
SEQ = 2048
DEPTH = 1

D_FF = 2816
CONV_WIDTH = 1024
CONV_HEADS = 16
CONV_K = 3
SSM_WIDTH = 512
SSM_GROUP = 16
SSM_GROUPS = SSM_WIDTH // SSM_GROUP
SSM_STATE = 64
DT_MIN = 0.001
DT_MAX = 0.1
N_MOD = 9
EPS = 1e-6
IN_COLS = 3 * CONV_WIDTH + SSM_WIDTH + 2 * D_MODEL

kernel_name = "hybrid_shortconv_s5_macaron_block"


def rms_norm(x, g):
    xf = x.astype(jnp.float32)
    y = xf * lax.rsqrt(jnp.mean(xf * xf, axis=-1, keepdims=True) + EPS)
    return (y * g.astype(jnp.float32)).astype(x.dtype)


def modulate(h, shift, scale):
    return h * (1.0 + scale[:, None, :]) + shift[:, None, :]


def swiglu(h, w1, w3, w2):
    return (jax.nn.silu(h @ w1) * (h @ w3)) @ w2


def short_conv(v, w):
    return lax.conv_general_dilated(
        v, w[:, None, :].astype(v.dtype), window_strides=(1,),
        padding=((CONV_K - 1, 0),),
        dimension_numbers=("NWC", "WIO", "NWC"),
        feature_group_count=v.shape[-1])


def s5_ssm(u, a_re, a_im, b_re, b_im, c_re, c_im, log_dt):
    bsz, seq, _ = u.shape
    f32 = jnp.float32
    uf = u.astype(f32).reshape(bsz, seq, SSM_GROUPS, SSM_GROUP).transpose(1, 0, 2, 3)
    lam = lax.complex(a_re.astype(f32), a_im.astype(f32))
    dt = jnp.exp(log_dt.astype(f32))[:, None]
    a_bar = jnp.exp(lam * dt)
    b_mat = lax.complex(b_re.astype(f32), b_im.astype(f32))
    b_bar = ((a_bar - 1.0) / lam)[..., None] * b_mat
    c_mat = lax.complex(c_re.astype(f32), c_im.astype(f32))
    bu = jnp.einsum("sbgh,gph->sbgp", uf.astype(jnp.complex64), b_bar)
    a_seq = jnp.broadcast_to(a_bar, (seq, 1) + a_bar.shape)

    def combine(left, right):
        a_l, b_l = left
        a_r, b_r = right
        return (a_r * a_l, a_r * b_l + b_r)

    _, states = lax.associative_scan(combine, (a_seq, bu), axis=0)
    y = jnp.einsum("sbgp,ghp->sbgh", states, c_mat).real
    return y.transpose(1, 0, 2, 3).reshape(bsz, seq, SSM_WIDTH)


def hybrid_mixer(h, w_in, conv_w, w_conv_out, a_re, a_im, b_re, b_im, c_re, c_im,
                 log_dt, d_skip, w_glu, w_ssm_out, w_out):
    proj = h @ w_in
    cuts = [CONV_WIDTH, 2 * CONV_WIDTH, 3 * CONV_WIDTH,
            3 * CONV_WIDTH + SSM_WIDTH, 3 * CONV_WIDTH + SSM_WIDTH + D_MODEL]
    gate_b, gate_c, v, u, glog_a, glog_b = jnp.split(proj, cuts, axis=-1)
    y_a = (gate_b * short_conv(gate_c * v, conv_w)) @ w_conv_out
    s = s5_ssm(u, a_re, a_im, b_re, b_im, c_re, c_im, log_dt).astype(u.dtype) + d_skip * u
    s = jax.nn.gelu(s)
    s = s * jax.nn.sigmoid(s @ w_glu)
    y_b = s @ w_ssm_out
    merged = jax.nn.sigmoid(glog_a) * y_a + jax.nn.sigmoid(glog_b) * y_b
    return merged @ w_out


def setup_inputs(seed: int = 0) -> dict:
    key = jax.random.key(seed)
    ks = jax.random.split(key, 32)
    L, D, G, P, H = DEPTH, D_MODEL, SSM_GROUPS, SSM_STATE, SSM_GROUP
    nrm = lambda k, shape, s: jax.random.normal(k, shape, jnp.float32) * s
    gain = lambda k: 1.0 + 0.02 * jax.random.normal(k, (L, D), jnp.float32)
    a_im_base = jnp.pi * jnp.arange(P, dtype=jnp.float32)
    return {
        "x": nrm(ks[0], (BATCH, SEQ, D), 1.0),
        "c": nrm(ks[1], (BATCH, D), 1.0),
        "w_ada": nrm(ks[2], (L, D, N_MOD * D), 0.5 * D ** -0.5),
        "b_ada": nrm(ks[3], (L, N_MOD * D), 0.01),
        "g_ffn1": gain(ks[4]),
        "w1_a": nrm(ks[5], (L, D, D_FF), D ** -0.5),
        "w3_a": nrm(ks[6], (L, D, D_FF), D ** -0.5),
        "w2_a": nrm(ks[7], (L, D_FF, D), D_FF ** -0.5),
        "g_mix": gain(ks[8]),
        "w_in": nrm(ks[9], (L, D, IN_COLS), D ** -0.5),
        "conv_w": nrm(ks[10], (L, CONV_K, CONV_WIDTH), CONV_K ** -0.5),
        "w_conv_out": nrm(ks[11], (L, CONV_WIDTH, D), CONV_WIDTH ** -0.5),
        "a_re": -0.5 + nrm(ks[12], (L, G, P), 0.01),
        "a_im": a_im_base + nrm(ks[13], (L, G, P), 0.01),
        "b_re": nrm(ks[14], (L, G, P, H), (2.0 * H) ** -0.5),
        "b_im": nrm(ks[15], (L, G, P, H), (2.0 * H) ** -0.5),
        "c_re": nrm(ks[16], (L, G, H, P), P ** -0.5),
        "c_im": nrm(ks[17], (L, G, H, P), P ** -0.5),
        "log_dt": jax.random.uniform(ks[18], (L, G), jnp.float32,
                                     math.log(DT_MIN), math.log(DT_MAX)),
        "d_skip": nrm(ks[19], (L, SSM_WIDTH), 1.0),
        "w_glu": nrm(ks[20], (L, SSM_WIDTH, SSM_WIDTH), SSM_WIDTH ** -0.5),
        "w_ssm_out": nrm(ks[21], (L, SSM_WIDTH, D), SSM_WIDTH ** -0.5),
        "w_out": nrm(ks[22], (L, D, D), D ** -0.5),
        "g_ffn2": gain(ks[23]),
        "w1_b": nrm(ks[24], (L, D, D_FF), D ** -0.5),
        "w3_b": nrm(ks[25], (L, D, D_FF), D ** -0.5),
        "w2_b": nrm(ks[26], (L, D_FF, D), D_FF ** -0.5),
        "g_final": 1.0 + 0.02 * jax.random.normal(ks[27], (D,), jnp.float32),
    }


def reference(x, c, w_ada, b_ada, g_ffn1, w1_a, w3_a, w2_a, g_mix, w_in, conv_w,
              w_conv_out, a_re, a_im, b_re, b_im, c_re, c_im, log_dt, d_skip, w_glu,
              w_ssm_out, w_out, g_ffn2, w1_b, w3_b, w2_b, g_final):
    cond = jax.nn.silu(c)
    for l in range(DEPTH):
        mod = cond @ w_ada[l] + b_ada[l]
        sh1, sc1, gt1, sh2, sc2, gt2, sh3, sc3, gt3 = jnp.split(mod, N_MOD, axis=-1)
        h = modulate(rms_norm(x, g_ffn1[l]), sh1, sc1)
        x = x + 0.5 * gt1[:, None, :] * swiglu(h, w1_a[l], w3_a[l], w2_a[l])
        h = modulate(rms_norm(x, g_mix[l]), sh2, sc2)
        x = x + gt2[:, None, :] * hybrid_mixer(
            h, w_in[l], conv_w[l], w_conv_out[l], a_re[l], a_im[l], b_re[l], b_im[l],
            c_re[l], c_im[l], log_dt[l], d_skip[l], w_glu[l], w_ssm_out[l], w_out[l])
        h = modulate(rms_norm(x, g_ffn2[l]), sh3, sc3)
        x = x + 0.5 * gt3[:, None, :] * swiglu(h, w1_b[l], w3_b[l], w2_b[l])
    return rms_norm(x, g_final)
```

```python
import functools

import jax
import jax.numpy as jnp
from jax import lax
from jax.experimental import pallas as pl
from jax.experimental.pallas import tpu as pltpu

D_MODEL = 1024
D_FF = 2816
CONV_WIDTH = 1024
CONV_K = 3
SSM_WIDTH = 512
SSM_GROUP = 16
SSM_GROUPS = SSM_WIDTH // SSM_GROUP
SSM_STATE = 64
N_MOD = 9
EPS = 1e-6

SSM_CHUNK = 16
SSM_ROW = SSM_CHUNK * SSM_GROUP
SSM_PAIR_STATE = 2 * SSM_STATE
CONV_HALO = 8

V7X_VMEM_BYTES = 64 * 1024 * 1024
VMEM_LIMIT = V7X_VMEM_BYTES - 8 * 1024 * 1024

TM_FFN = 512
TM_MIX = 256
TN_MOD = 1024

BF16 = jnp.bfloat16
F32 = jnp.float32


def _dot(a, b):
    return jnp.dot(a, b, preferred_element_type=F32)


def _const_spec(shape):
    zeros = (0,) * len(shape)
    return pl.BlockSpec(shape, lambda *_: zeros, pipeline_mode=pl.Buffered(1))


def _norm_mod(x, g, shift, scale):
    y = x * lax.rsqrt(jnp.mean(x * x, axis=-1, keepdims=True) + EPS)
    return (y * g) * (1.0 + scale) + shift


def _split_bf16(v):
    hi = v.astype(BF16)
    lo = (v - hi.astype(F32)).astype(BF16)
    return hi, lo


def _mod_kernel(c_ref, w_ref, b_ref, o_ref):
    c = c_ref[...]
    cond = c * jax.nn.sigmoid(c)
    c_hi, c_lo = _split_bf16(cond)
    w_hi, w_lo = _split_bf16(w_ref[...])
    acc = _dot(c_hi, w_hi) + (_dot(c_hi, w_lo) + _dot(c_lo, w_hi))
    o_ref[...] = acc + b_ref[...]


def _mod_call(c, w_ada, b_ada):
    bsz, d = c.shape
    n = w_ada.shape[1]
    return pl.pallas_call(
        _mod_kernel,
        grid=(n // TN_MOD,),
        in_specs=[
            pl.BlockSpec((bsz, d), lambda j: (0, 0)),
            pl.BlockSpec((d, TN_MOD), lambda j: (0, j)),
            pl.BlockSpec((1, TN_MOD), lambda j: (0, j)),
        ],
        out_specs=pl.BlockSpec((bsz, TN_MOD), lambda j: (0, j)),
        out_shape=jax.ShapeDtypeStruct((bsz, n), F32),
        compiler_params=pltpu.CompilerParams(dimension_semantics=("arbitrary",)),
        name="mod",
    )(c, w_ada, b_ada.reshape(1, n))


def _swiglu_update(x, m, row, g, w1_ref, w3_ref, w2_ref):
    h = _norm_mod(x, g, m[row:row + 1], m[row + 1:row + 2]).astype(BF16)
    a = _dot(h, w1_ref[...])
    b = _dot(h, w3_ref[...])
    act = ((a * jax.nn.sigmoid(a)) * b).astype(BF16)
    y = _dot(act, w2_ref[...])
    return x + (0.5 * m[row + 2:row + 3]) * y


def _ffn_a_kernel(x_ref, mod_ref, g1_ref, g2_ref, w1_ref, w3_ref, w2_ref, wu_ref,
                  x1_ref, u_ref, ub_ref):
    m = mod_ref[0]
    x1 = _swiglu_update(x_ref[0], m, 0, g1_ref[...], w1_ref, w3_ref, w2_ref)
    x1_ref[0] = x1
    h2 = _norm_mod(x1, g2_ref[...], m[3:4], m[4:5]).astype(BF16)
    u = _dot(h2, wu_ref[...])
    u_ref[0] = u
    ub_ref[0] = u.astype(BF16)


def _ffn_b_kernel(x_ref, mod_ref, g_ref, gf_ref, w1_ref, w3_ref, w2_ref, o_ref):
    x3 = _swiglu_update(x_ref[0], mod_ref[0], 6, g_ref[...], w1_ref, w3_ref, w2_ref)
    y = x3 * lax.rsqrt(jnp.mean(x3 * x3, axis=-1, keepdims=True) + EPS)
    o_ref[0] = y * gf_ref[...]


def _token_spec(tm, width):
    return pl.BlockSpec((1, tm, width), lambda b, j: (b, j, 0))


def _mod_spec():
    return pl.BlockSpec((1, N_MOD, D_MODEL), lambda b, j: (b, 0, 0))


def _ffn_a_call(x, mod3, g1, g2, w1, w3, w2, wu):
    bsz, seq, d = x.shape
    return pl.pallas_call(
        _ffn_a_kernel,
        grid=(bsz, seq // TM_FFN),
        in_specs=[
            _token_spec(TM_FFN, d), _mod_spec(), _const_spec((1, d)), _const_spec((1, d)),
            _const_spec(w1.shape), _const_spec(w3.shape), _const_spec(w2.shape), _const_spec(wu.shape),
        ],
        out_specs=[_token_spec(TM_FFN, d), _token_spec(TM_FFN, SSM_WIDTH), _token_spec(TM_FFN, SSM_WIDTH)],
        out_shape=[
            jax.ShapeDtypeStruct((bsz, seq, d), F32),
            jax.ShapeDtypeStruct((bsz, seq, SSM_WIDTH), F32),
            jax.ShapeDtypeStruct((bsz, seq, SSM_WIDTH), BF16),
        ],
        compiler_params=pltpu.CompilerParams(
            dimension_semantics=("arbitrary", "arbitrary"), vmem_limit_bytes=VMEM_LIMIT),
        name="ffn_a",
    )(x, mod3, g1, g2, w1, w3, w2, wu)


def _ffn_b_call(x, mod3, g, gf, w1, w3, w2):
    bsz, seq, d = x.shape
    return pl.pallas_call(
        _ffn_b_kernel,
        grid=(bsz, seq // TM_FFN),
        in_specs=[
            _token_spec(TM_FFN, d), _mod_spec(), _const_spec((1, d)), _const_spec((1, d)),
            _const_spec(w1.shape), _const_spec(w3.shape), _const_spec(w2.shape),
        ],
        out_specs=_token_spec(TM_FFN, d),
        out_shape=jax.ShapeDtypeStruct((bsz, seq, d), F32),
        compiler_params=pltpu.CompilerParams(
            dimension_semantics=("arbitrary", "arbitrary"), vmem_limit_bytes=VMEM_LIMIT),
        name="ffn_b",
    )(x, mod3, g, gf, w1, w3, w2)


def _ssm_operators(a_re, a_im, b_re, b_im, c_re, c_im, log_dt):
    g_n, p_n, h_n, t_n = SSM_GROUPS, SSM_STATE, SSM_GROUP, SSM_CHUNK
    hp = lax.Precision.HIGHEST
    dt = jnp.exp(log_dt)[:, None]
    lr, li = a_re * dt, a_im * dt
    tau = jnp.arange(t_n + 1, dtype=F32)[:, None, None]
    mag = jnp.exp(lr[None] * tau)
    pw_re, pw_im = mag * jnp.cos(li[None] * tau), mag * jnp.sin(li[None] * tau)
    nr, ni = pw_re[1] - 1.0, pw_im[1]
    den = a_re * a_re + a_im * a_im
    q_re, q_im = (nr * a_re + ni * a_im) / den, (ni * a_re - nr * a_im) / den
    bb_re = q_re[..., None] * b_re - q_im[..., None] * b_im
    bb_im = q_re[..., None] * b_im + q_im[..., None] * b_re
    cpt = jnp.swapaxes(c_re, 1, 2), jnp.swapaxes(c_im, 1, 2)
    cp_re = pw_re[..., None] * cpt[0][None] - pw_im[..., None] * cpt[1][None]
    cp_im = pw_re[..., None] * cpt[1][None] + pw_im[..., None] * cpt[0][None]
    k_lag = (jnp.einsum("gpi,tgpo->tgio", bb_re, cp_re[:t_n], precision=hp)
             - jnp.einsum("gpi,tgpo->tgio", bb_im, cp_im[:t_n], precision=hp))
    s_idx = jnp.arange(t_n)[:, None]
    t_idx = jnp.arange(t_n)[None, :]
    lag = jnp.clip(t_idx - s_idx, 0, t_n - 1)
    m_op = jnp.where((t_idx >= s_idx)[:, :, None, None, None], k_lag[lag], 0.0)
    m_op = m_op.transpose(2, 0, 3, 1, 4).reshape(g_n, SSM_ROW, SSM_ROW)
    rev = pw_re[t_n - 1::-1][:t_n], pw_im[t_n - 1::-1][:t_n]
    w_re = rev[0][..., None] * bb_re[None] - rev[1][..., None] * bb_im[None]
    w_im = rev[0][..., None] * bb_im[None] + rev[1][..., None] * bb_re[None]
    w_re = w_re.transpose(1, 0, 3, 2).reshape(g_n // 2, 2, SSM_ROW, p_n)
    w_im = w_im.transpose(1, 0, 3, 2).reshape(g_n // 2, 2, SSM_ROW, p_n)
    z = jnp.zeros_like(w_re[:, 0])
    w_op = jnp.concatenate([
        jnp.concatenate([w_re[:, 0], z, w_im[:, 0], z], axis=-1),
        jnp.concatenate([z, w_re[:, 1], z, w_im[:, 1]], axis=-1)], axis=1)
    o_re = cp_re[1:].transpose(1, 2, 0, 3).reshape(g_n // 2, 2, p_n, SSM_ROW)
    o_im = (-cp_im[1:]).transpose(1, 2, 0, 3).reshape(g_n // 2, 2, p_n, SSM_ROW)
    zo = jnp.zeros_like(o_re[:, 0])
    o_op = jnp.concatenate([
        jnp.concatenate([o_re[:, 0], zo], axis=-1), jnp.concatenate([zo, o_re[:, 1]], axis=-1),
        jnp.concatenate([o_im[:, 0], zo], axis=-1), jnp.concatenate([zo, o_im[:, 1]], axis=-1)],
        axis=1)
    a_op = jnp.stack([pw_re[t_n].reshape(g_n // 2, SSM_PAIR_STATE),
                      pw_im[t_n].reshape(g_n // 2, SSM_PAIR_STATE)], axis=1)
    return m_op.astype(BF16), w_op.astype(BF16), o_op.astype(BF16), a_op


def _ssm_kernel(u_ref, m_ref, w_ref, o_ref, a_ref, y_ref, wu_ref, hs_ref, *, bsz, n_chunks):
    ps = SSM_PAIR_STATE
    u0 = u_ref[0]
    u1 = u_ref[1]
    wu_ref[...] = _dot(u0, w_ref[0, 0:SSM_ROW, :]) + _dot(u1, w_ref[0, SSM_ROW:2 * SSM_ROW, :])
    a_re = jnp.broadcast_to(a_ref[0, 0:1, :], (bsz, ps))
    a_im = jnp.broadcast_to(a_ref[0, 1:2, :], (bsz, ps))

    def step(k, carry):
        h_re, h_im = carry
        r = pl.multiple_of(k * bsz, bsz)
        hs_ref[pl.ds(r, bsz), 0:ps] = h_re.astype(BF16)
        hs_ref[pl.ds(r, bsz), ps:2 * ps] = h_im.astype(BF16)
        n_re = (a_re * h_re - a_im * h_im) + wu_ref[pl.ds(r, bsz), 0:ps]
        n_im = (a_re * h_im + a_im * h_re) + wu_ref[pl.ds(r, bsz), ps:2 * ps]
        return n_re, n_im

    zero = jnp.zeros((bsz, ps), F32)
    lax.fori_loop(0, n_chunks, step, (zero, zero))
    hs = hs_ref[...]
    y_ref[0] = _dot(u0, m_ref[0]) + _dot(hs, o_ref[0, :, 0:SSM_ROW])
    y_ref[1] = _dot(u1, m_ref[1]) + _dot(hs, o_ref[0, :, SSM_ROW:2 * SSM_ROW])


def _ssm_call(u_bf16, ops):
    bsz, seq, _ = u_bf16.shape
    n_chunks = seq // SSM_CHUNK
    rows = n_chunks * bsz
    m_op, w_op, o_op, a_op = ops
    uc = u_bf16.reshape(bsz, n_chunks, SSM_CHUNK, SSM_GROUPS, SSM_GROUP)
    uc = uc.transpose(3, 1, 0, 2, 4).reshape(SSM_GROUPS, rows, SSM_ROW)
    yc = pl.pallas_call(
        functools.partial(_ssm_kernel, bsz=bsz, n_chunks=n_chunks),
        grid=(SSM_GROUPS // 2,),
        in_specs=[
            pl.BlockSpec((2, rows, SSM_ROW), lambda p: (p, 0, 0)),
            pl.BlockSpec((2, SSM_ROW, SSM_ROW), lambda p: (p, 0, 0)),
            pl.BlockSpec((1, 2 * SSM_ROW, 2 * SSM_PAIR_STATE), lambda p: (p, 0, 0)),
            pl.BlockSpec((1, 2 * SSM_PAIR_STATE, 2 * SSM_ROW), lambda p: (p, 0, 0)),
            pl.BlockSpec((1, 2, SSM_PAIR_STATE), lambda p: (p, 0, 0)),
        ],
        out_specs=pl.BlockSpec((2, rows, SSM_ROW), lambda p: (p, 0, 0)),
        out_shape=jax.ShapeDtypeStruct((SSM_GROUPS, rows, SSM_ROW), F32),
        scratch_shapes=[
            pltpu.VMEM((rows, 2 * SSM_PAIR_STATE), F32),
            pltpu.VMEM((rows, 2 * SSM_PAIR_STATE), BF16),
        ],
        compiler_params=pltpu.CompilerParams(
            dimension_semantics=("arbitrary",), vmem_limit_bytes=VMEM_LIMIT),
        name="ssm",
    )(uc, m_op, w_op, o_op, a_op)
    y = yc.reshape(SSM_GROUPS, n_chunks, bsz, SSM_CHUNK, SSM_GROUP)
    return y.transpose(2, 1, 3, 0, 4).reshape(bsz, seq, SSM_WIDTH)


_GB, _GC, _V = 0, CONV_WIDTH, 2 * CONV_WIDTH
_U = 3 * CONV_WIDTH
_GA = _U + SSM_WIDTH
_GBB = _GA + D_MODEL


def _mixer_kernel(x_ref, u_ref, ys_ref, mod_ref, g_ref, win_ref, cw_ref, wco_ref, dsk_ref,
                  wglu_ref, wso_ref, wout_ref, o_ref, cv_ref):
    tm = x_ref.shape[1]
    hl = CONV_HALO
    m = mod_ref[0]
    x1 = x_ref[0]
    h2 = _norm_mod(x1, g_ref[...], m[3:4], m[4:5]).astype(BF16)

    @pl.when(pl.program_id(1) == 0)
    def _():
        cv_ref[0:hl, :] = jnp.zeros((hl, CONV_WIDTH), F32)

    cv = _dot(h2, win_ref[:, _GC:_GC + CONV_WIDTH]) * _dot(h2, win_ref[:, _V:_V + CONV_WIDTH])
    cv_ref[hl:hl + tm, :] = cv
    cw = cw_ref[...]
    conv = (cw[2:3] * cv + cw[1:2] * cv_ref[hl - 1:hl - 1 + tm, :]) + cw[0:1] * cv_ref[hl - 2:hl - 2 + tm, :]
    cv_ref[0:hl, :] = cv_ref[tm:tm + hl, :]
    gate_b = _dot(h2, win_ref[:, _GB:_GB + CONV_WIDTH])
    y_a = _dot((gate_b * conv).astype(BF16), wco_ref[...])

    s = jax.nn.gelu(ys_ref[0] + dsk_ref[...] * u_ref[0])
    s = s * jax.nn.sigmoid(_dot(s.astype(BF16), wglu_ref[...]))
    y_b = _dot(s.astype(BF16), wso_ref[...])

    glog_a = _dot(h2, win_ref[:, _GA:_GA + D_MODEL])
    glog_b = _dot(h2, win_ref[:, _GBB:_GBB + D_MODEL])
    merged = jax.nn.sigmoid(glog_a) * y_a + jax.nn.sigmoid(glog_b) * y_b
    o_ref[0] = x1 + m[5:6] * _dot(merged.astype(BF16), wout_ref[...])


def _mixer_call(x1, u, ys, mod3, g, w_in, conv_w, w_conv_out, d_skip, w_glu, w_ssm_out, w_out):
    bsz, seq, d = x1.shape
    return pl.pallas_call(
        _mixer_kernel,
        grid=(bsz, seq // TM_MIX),
        in_specs=[
            _token_spec(TM_MIX, d), _token_spec(TM_MIX, SSM_WIDTH), _token_spec(TM_MIX, SSM_WIDTH),
            _mod_spec(), _const_spec((1, d)), _const_spec(w_in.shape), _const_spec(conv_w.shape),
            _const_spec(w_conv_out.shape), _const_spec((1, SSM_WIDTH)), _const_spec(w_glu.shape),
            _const_spec(w_ssm_out.shape), _const_spec(w_out.shape),
        ],
        out_specs=_token_spec(TM_MIX, d),
        out_shape=jax.ShapeDtypeStruct((bsz, seq, d), F32),
        scratch_shapes=[pltpu.VMEM((TM_MIX + CONV_HALO, CONV_WIDTH), F32)],
        compiler_params=pltpu.CompilerParams(
            dimension_semantics=("arbitrary", "arbitrary"), vmem_limit_bytes=VMEM_LIMIT),
        name="mixer",
    )(x1, u, ys, mod3, g, w_in, conv_w, w_conv_out, d_skip, w_glu, w_ssm_out, w_out)


def kernel(x, c, w_ada, b_ada, g_ffn1, w1_a, w3_a, w2_a, g_mix, w_in, conv_w, w_conv_out, a_re, a_im,
           b_re, b_im, c_re, c_im, log_dt, d_skip, w_glu, w_ssm_out, w_out, g_ffn2, w1_b, w3_b, w2_b,
           g_final):
    assert w_ada.shape[0] == 1, "the final norm is fused into the last FFN: single-layer stacks only"
    bsz = x.shape[0]
    bf = lambda w: w[0].astype(BF16)
    mod3 = _mod_call(c, w_ada[0], b_ada[0]).reshape(bsz, N_MOD, D_MODEL)
    w_in_l = bf(w_in)
    x1, u, u_bf16 = _ffn_a_call(
        x, mod3, g_ffn1, g_mix, bf(w1_a), bf(w3_a), bf(w2_a), w_in_l[:, _U:_U + SSM_WIDTH])
    ops = _ssm_operators(a_re[0], a_im[0], b_re[0], b_im[0], c_re[0], c_im[0], log_dt[0])
    ys = _ssm_call(u_bf16, ops)
    x2 = _mixer_call(x1, u, ys, mod3, g_mix, w_in_l, conv_w[0], bf(w_conv_out), d_skip,
                     bf(w_glu), bf(w_ssm_out), bf(w_out))
    return _ffn_b_call(x2, mod3, g_ffn2, g_final[None], bf(w1_b), bf(w3_b), bf(w2_b))
```

```python
import functools

import jax
import jax.numpy as jnp
from jax import lax
from jax.experimental import pallas as pl
from jax.experimental.pallas import tpu as pltpu

D_MODEL = 1024
D_FF = 2816
CONV_WIDTH = 1024
CONV_K = 3
SSM_WIDTH = 512
SSM_GROUP = 16
SSM_GROUPS = SSM_WIDTH // SSM_GROUP
SSM_STATE = 64
N_MOD = 9
EPS = 1e-6

SSM_CHUNK = 16
SSM_ROW = SSM_CHUNK * SSM_GROUP
SSM_PAIR_STATE = 2 * SSM_STATE
LANES = 128
BLOCKS_PER_VREG = LANES // SSM_GROUP
CONV_HALO = 8

V7X_VMEM_BYTES = 64 * 1024 * 1024
VMEM_LIMIT = V7X_VMEM_BYTES - 8 * 1024 * 1024

TM_FFN = 512
TM_MIX = 256
TN_MOD = 1024

BF16 = jnp.bfloat16
F32 = jnp.float32


def _dot(a, b):
    return jnp.dot(a, b, preferred_element_type=F32)


def _const_spec(shape):
    zeros = (0,) * len(shape)
    return pl.BlockSpec(shape, lambda *_: zeros, pipeline_mode=pl.Buffered(1))


def _norm_mod(x, g, shift, scale):
    y = x * lax.rsqrt(jnp.mean(x * x, axis=-1, keepdims=True) + EPS)
    return (y * g) * (1.0 + scale) + shift


def _split_bf16(v):
    hi = v.astype(BF16)
    lo = (v - hi.astype(F32)).astype(BF16)
    return hi, lo


def _mod_kernel(c_ref, w_ref, b_ref, o_ref):
    c = c_ref[...]
    cond = c * jax.nn.sigmoid(c)
    c_hi, c_lo = _split_bf16(cond)
    w_hi, w_lo = _split_bf16(w_ref[...])
    acc = _dot(c_hi, w_hi) + (_dot(c_hi, w_lo) + _dot(c_lo, w_hi))
    o_ref[...] = acc + b_ref[...]


def _mod_call(c, w_ada, b_ada):
    bsz, d = c.shape
    n = w_ada.shape[1]
    return pl.pallas_call(
        _mod_kernel,
        grid=(n // TN_MOD,),
        in_specs=[
            pl.BlockSpec((bsz, d), lambda j: (0, 0)),
            pl.BlockSpec((d, TN_MOD), lambda j: (0, j)),
            pl.BlockSpec((1, TN_MOD), lambda j: (0, j)),
        ],
        out_specs=pl.BlockSpec((bsz, TN_MOD), lambda j: (0, j)),
        out_shape=jax.ShapeDtypeStruct((bsz, n), F32),
        compiler_params=pltpu.CompilerParams(dimension_semantics=("arbitrary",)),
        name="mod",
    )(c, w_ada, b_ada.reshape(1, n))


def _swiglu_update(x, m, row, g, w1_ref, w3_ref, w2_ref):
    h = _norm_mod(x, g, m[row:row + 1], m[row + 1:row + 2]).astype(BF16)
    a = _dot(h, w1_ref[...])
    b = _dot(h, w3_ref[...])
    act = ((a * jax.nn.sigmoid(a)) * b).astype(BF16)
    y = _dot(act, w2_ref[...])
    return x + (0.5 * m[row + 2:row + 3]) * y


def _lane_block_masks(rows):
    blk = lax.broadcasted_iota(jnp.int32, (rows, LANES), 1) // SSM_GROUP
    return [blk == i for i in range(BLOCKS_PER_VREG)]


def _interleave_blocks(pieces, src, masks):
    acc = None
    for i, piece in enumerate(pieces):
        shift = ((i - src) % BLOCKS_PER_VREG) * SSM_GROUP
        if shift:
            piece = pltpu.roll(piece, shift, 1)
        acc = piece if acc is None else jnp.where(masks[i], piece, acc)
    return acc


def _ffn_a_kernel(x_ref, mod_ref, g1_ref, g2_ref, w1_ref, w3_ref, w2_ref, wu_ref,
                  x1_ref, u_ref, uc_ref, slab_ref):
    m = mod_ref[0]
    x1 = _swiglu_update(x_ref[0], m, 0, g1_ref[...], w1_ref, w3_ref, w2_ref)
    x1_ref[0] = x1
    h2 = _norm_mod(x1, g2_ref[...], m[3:4], m[4:5]).astype(BF16)
    u = _dot(h2, wu_ref[...])
    u_ref[0] = u
    n_rows = u.shape[0] // SSM_CHUNK
    masks = _lane_block_masks(n_rows)
    for v in range(SSM_WIDTH // LANES):
        slab_ref[v] = u[:, v * LANES:(v + 1) * LANES]
    for v in range(SSM_WIDTH // LANES):
        for j in range(SSM_ROW // LANES):
            toks = [slab_ref[v, pl.ds(BLOCKS_PER_VREG * j + sb, n_rows, stride=SSM_CHUNK), :]
                    for sb in range(BLOCKS_PER_VREG)]
            for gb in range(BLOCKS_PER_VREG):
                row = _interleave_blocks(toks, gb, masks)
                uc_ref[BLOCKS_PER_VREG * v + gb, 0, :, j * LANES:(j + 1) * LANES] = row.astype(BF16)


def _ffn_b_kernel(x_ref, mod_ref, g_ref, gf_ref, w1_ref, w3_ref, w2_ref, o_ref):
    x3 = _swiglu_update(x_ref[0], mod_ref[0], 6, g_ref[...], w1_ref, w3_ref, w2_ref)
    y = x3 * lax.rsqrt(jnp.mean(x3 * x3, axis=-1, keepdims=True) + EPS)
    o_ref[0] = y * gf_ref[...]


def _token_spec(tm, width):
    return pl.BlockSpec((1, tm, width), lambda b, j: (b, j, 0))


def _mod_spec():
    return pl.BlockSpec((1, N_MOD, D_MODEL), lambda b, j: (b, 0, 0))


def _chunk_spec(tm):
    return pl.BlockSpec((SSM_GROUPS, 1, tm // SSM_CHUNK, SSM_ROW), lambda b, j: (0, b, j, 0))


def _ffn_a_call(x, mod3, g1, g2, w1, w3, w2, wu):
    bsz, seq, d = x.shape
    return pl.pallas_call(
        _ffn_a_kernel,
        grid=(bsz, seq // TM_FFN),
        in_specs=[
            _token_spec(TM_FFN, d), _mod_spec(), _const_spec((1, d)), _const_spec((1, d)),
            _const_spec(w1.shape), _const_spec(w3.shape), _const_spec(w2.shape), _const_spec(wu.shape),
        ],
        out_specs=[_token_spec(TM_FFN, d), _token_spec(TM_FFN, SSM_WIDTH), _chunk_spec(TM_FFN)],
        out_shape=[
            jax.ShapeDtypeStruct((bsz, seq, d), F32),
            jax.ShapeDtypeStruct((bsz, seq, SSM_WIDTH), F32),
            jax.ShapeDtypeStruct((SSM_GROUPS, bsz, seq // SSM_CHUNK, SSM_ROW), BF16),
        ],
        scratch_shapes=[pltpu.VMEM((SSM_WIDTH // LANES, TM_FFN, LANES), F32)],
        compiler_params=pltpu.CompilerParams(
            dimension_semantics=("arbitrary", "arbitrary"), vmem_limit_bytes=VMEM_LIMIT),
        name="ffn_a",
    )(x, mod3, g1, g2, w1, w3, w2, wu)


def _ffn_b_call(x, mod3, g, gf, w1, w3, w2):
    bsz, seq, d = x.shape
    return pl.pallas_call(
        _ffn_b_kernel,
        grid=(bsz, seq // TM_FFN),
        in_specs=[
            _token_spec(TM_FFN, d), _mod_spec(), _const_spec((1, d)), _const_spec((1, d)),
            _const_spec(w1.shape), _const_spec(w3.shape), _const_spec(w2.shape),
        ],
        out_specs=_token_spec(TM_FFN, d),
        out_shape=jax.ShapeDtypeStruct((bsz, seq, d), F32),
        compiler_params=pltpu.CompilerParams(
            dimension_semantics=("arbitrary", "arbitrary"), vmem_limit_bytes=VMEM_LIMIT),
        name="ffn_b",
    )(x, mod3, g, gf, w1, w3, w2)


def _ssm_operators(a_re, a_im, b_re, b_im, c_re, c_im, log_dt):
    g_n, p_n, t_n = SSM_GROUPS, SSM_STATE, SSM_CHUNK
    hp = lax.Precision.HIGHEST
    dt = jnp.exp(log_dt)[:, None]
    lr, li = a_re * dt, a_im * dt
    tau = jnp.arange(t_n + 1, dtype=F32)[:, None, None]
    mag = jnp.exp(lr[None] * tau)
    pw_re, pw_im = mag * jnp.cos(li[None] * tau), mag * jnp.sin(li[None] * tau)
    nr, ni = pw_re[1] - 1.0, pw_im[1]
    den = a_re * a_re + a_im * a_im
    q_re, q_im = (nr * a_re + ni * a_im) / den, (ni * a_re - nr * a_im) / den
    bb_re = q_re[..., None] * b_re - q_im[..., None] * b_im
    bb_im = q_re[..., None] * b_im + q_im[..., None] * b_re
    cpt = jnp.swapaxes(c_re, 1, 2), jnp.swapaxes(c_im, 1, 2)
    cp_re = pw_re[..., None] * cpt[0][None] - pw_im[..., None] * cpt[1][None]
    cp_im = pw_re[..., None] * cpt[1][None] + pw_im[..., None] * cpt[0][None]
    k_lag = (jnp.einsum("gpi,tgpo->tgio", bb_re, cp_re[:t_n], precision=hp)
             - jnp.einsum("gpi,tgpo->tgio", bb_im, cp_im[:t_n], precision=hp))
    s_idx = jnp.arange(t_n)[:, None]
    t_idx = jnp.arange(t_n)[None, :]
    lag = jnp.clip(t_idx - s_idx, 0, t_n - 1)
    m_op = jnp.where((t_idx >= s_idx)[:, :, None, None, None], k_lag[lag], 0.0)
    m_op = m_op.transpose(2, 0, 3, 1, 4).reshape(g_n, SSM_ROW, SSM_ROW)
    rev = pw_re[t_n - 1::-1][:t_n], pw_im[t_n - 1::-1][:t_n]
    w_re = rev[0][..., None] * bb_re[None] - rev[1][..., None] * bb_im[None]
    w_im = rev[0][..., None] * bb_im[None] + rev[1][..., None] * bb_re[None]
    w_re = w_re.transpose(1, 0, 3, 2).reshape(g_n // 2, 2, SSM_ROW, p_n)
    w_im = w_im.transpose(1, 0, 3, 2).reshape(g_n // 2, 2, SSM_ROW, p_n)
    z = jnp.zeros_like(w_re[:, 0])
    w_op = jnp.concatenate([
        jnp.concatenate([w_re[:, 0], z, w_im[:, 0], z], axis=-1),
        jnp.concatenate([z, w_re[:, 1], z, w_im[:, 1]], axis=-1)], axis=1)
    o_re = cp_re[1:].transpose(1, 2, 0, 3).reshape(g_n // 2, 2, p_n, SSM_ROW)
    o_im = (-cp_im[1:]).transpose(1, 2, 0, 3).reshape(g_n // 2, 2, p_n, SSM_ROW)
    zo = jnp.zeros_like(o_re[:, 0])
    o_op = jnp.concatenate([
        jnp.concatenate([o_re[:, 0], zo], axis=-1), jnp.concatenate([zo, o_re[:, 1]], axis=-1),
        jnp.concatenate([o_im[:, 0], zo], axis=-1), jnp.concatenate([zo, o_im[:, 1]], axis=-1)],
        axis=1)
    a_op = jnp.stack([pw_re[t_n].reshape(g_n // 2, SSM_PAIR_STATE),
                      pw_im[t_n].reshape(g_n // 2, SSM_PAIR_STATE)], axis=1)
    return m_op.astype(BF16), w_op.astype(BF16), o_op.astype(BF16), a_op


def _ssm_kernel(u_ref, m_ref, w_ref, o_ref, a_ref, y_ref, wu_ref, hs_ref, *, bsz, n_chunks):
    ps = SSM_PAIR_STATE
    u0 = u_ref[0]
    u1 = u_ref[1]
    wu = _dot(u0, w_ref[0, 0:SSM_ROW, :]) + _dot(u1, w_ref[0, SSM_ROW:2 * SSM_ROW, :])
    wu_ref[0] = wu[:, 0:ps]
    wu_ref[1] = wu[:, ps:2 * ps]
    a_re = jnp.broadcast_to(a_ref[0, 0:1, :], (bsz, ps))
    a_im = jnp.broadcast_to(a_ref[0, 1:2, :], (bsz, ps))

    def step(k, carry):
        h_re, h_im = carry
        rows_k = pl.ds(k, bsz, stride=n_chunks)
        hs_ref[0, rows_k, :] = h_re
        hs_ref[1, rows_k, :] = h_im
        n_re = (a_re * h_re - a_im * h_im) + wu_ref[0, rows_k, :]
        n_im = (a_re * h_im + a_im * h_re) + wu_ref[1, rows_k, :]
        return n_re, n_im

    zero = jnp.zeros((bsz, ps), F32)
    lax.fori_loop(0, n_chunks, step, (zero, zero))
    hs = jnp.concatenate([hs_ref[0], hs_ref[1]], axis=1).astype(BF16)
    y_ref[0] = _dot(u0, m_ref[0]) + _dot(hs, o_ref[0, :, 0:SSM_ROW])
    y_ref[1] = _dot(u1, m_ref[1]) + _dot(hs, o_ref[0, :, SSM_ROW:2 * SSM_ROW])


def _ssm_call(uc, ops):
    g_n, bsz, n_chunks, _ = uc.shape
    rows = n_chunks * bsz
    m_op, w_op, o_op, a_op = ops
    yc = pl.pallas_call(
        functools.partial(_ssm_kernel, bsz=bsz, n_chunks=n_chunks),
        grid=(g_n // 2,),
        in_specs=[
            pl.BlockSpec((2, rows, SSM_ROW), lambda p: (p, 0, 0)),
            pl.BlockSpec((2, SSM_ROW, SSM_ROW), lambda p: (p, 0, 0)),
            pl.BlockSpec((1, 2 * SSM_ROW, 2 * SSM_PAIR_STATE), lambda p: (p, 0, 0)),
            pl.BlockSpec((1, 2 * SSM_PAIR_STATE, 2 * SSM_ROW), lambda p: (p, 0, 0)),
            pl.BlockSpec((1, 2, SSM_PAIR_STATE), lambda p: (p, 0, 0)),
        ],
        out_specs=pl.BlockSpec((2, rows, SSM_ROW), lambda p: (p, 0, 0)),
        out_shape=jax.ShapeDtypeStruct((g_n, rows, SSM_ROW), F32),
        scratch_shapes=[
            pltpu.VMEM((2, rows, SSM_PAIR_STATE), F32),
            pltpu.VMEM((2, rows, SSM_PAIR_STATE), F32),
        ],
        compiler_params=pltpu.CompilerParams(
            dimension_semantics=("arbitrary",), vmem_limit_bytes=VMEM_LIMIT),
        name="ssm",
    )(uc.reshape(g_n, rows, SSM_ROW), m_op, w_op, o_op, a_op)
    return yc.reshape(g_n, bsz, n_chunks, SSM_ROW)


_GB, _GC, _V = 0, CONV_WIDTH, 2 * CONV_WIDTH
_U = 3 * CONV_WIDTH
_GA = _U + SSM_WIDTH
_GBB = _GA + D_MODEL


def _mixer_kernel(x_ref, u_ref, yc_ref, mod_ref, g_ref, win_ref, cw_ref, wco_ref, dsk_ref,
                  wglu_ref, wso_ref, wout_ref, o_ref, cv_ref, slab_ref):
    tm = x_ref.shape[1]
    hl = CONV_HALO
    m = mod_ref[0]
    x1 = x_ref[0]
    h2 = _norm_mod(x1, g_ref[...], m[3:4], m[4:5]).astype(BF16)

    @pl.when(pl.program_id(1) == 0)
    def _():
        cv_ref[0:hl, :] = jnp.zeros((hl, CONV_WIDTH), F32)

    cv = _dot(h2, win_ref[:, _GC:_GC + CONV_WIDTH]) * _dot(h2, win_ref[:, _V:_V + CONV_WIDTH])
    cv_ref[hl:hl + tm, :] = cv
    cw = cw_ref[...]
    conv = (cw[2:3] * cv + cw[1:2] * cv_ref[hl - 1:hl - 1 + tm, :]) + cw[0:1] * cv_ref[hl - 2:hl - 2 + tm, :]
    cv_ref[0:hl, :] = cv_ref[tm:tm + hl, :]
    gate_b = _dot(h2, win_ref[:, _GB:_GB + CONV_WIDTH])
    y_a = _dot((gate_b * conv).astype(BF16), wco_ref[...])

    n_rows = tm // SSM_CHUNK
    masks = _lane_block_masks(n_rows)
    for v in range(SSM_WIDTH // LANES):
        for j in range(SSM_ROW // LANES):
            grps = [yc_ref[BLOCKS_PER_VREG * v + gb, 0, :, j * LANES:(j + 1) * LANES]
                    for gb in range(BLOCKS_PER_VREG)]
            for tb in range(BLOCKS_PER_VREG):
                slab_ref[v, pl.ds(BLOCKS_PER_VREG * j + tb, n_rows, stride=SSM_CHUNK), :] = (
                    _interleave_blocks(grps, tb, masks))
    ys = jnp.concatenate([slab_ref[v] for v in range(SSM_WIDTH // LANES)], axis=1)

    s = jax.nn.gelu(ys + dsk_ref[...] * u_ref[0])
    s = s * jax.nn.sigmoid(_dot(s.astype(BF16), wglu_ref[...]))
    y_b = _dot(s.astype(BF16), wso_ref[...])

    glog_a = _dot(h2, win_ref[:, _GA:_GA + D_MODEL])
    glog_b = _dot(h2, win_ref[:, _GBB:_GBB + D_MODEL])
    merged = jax.nn.sigmoid(glog_a) * y_a + jax.nn.sigmoid(glog_b) * y_b
    o_ref[0] = x1 + m[5:6] * _dot(merged.astype(BF16), wout_ref[...])


def _mixer_call(x1, u, yc, mod3, g, w_in, conv_w, w_conv_out, d_skip, w_glu, w_ssm_out, w_out):
    bsz, seq, d = x1.shape
    return pl.pallas_call(
        _mixer_kernel,
        grid=(bsz, seq // TM_MIX),
        in_specs=[
            _token_spec(TM_MIX, d), _token_spec(TM_MIX, SSM_WIDTH), _chunk_spec(TM_MIX),
            _mod_spec(), _const_spec((1, d)), _const_spec(w_in.shape), _const_spec(conv_w.shape),
            _const_spec(w_conv_out.shape), _const_spec((1, SSM_WIDTH)), _const_spec(w_glu.shape),
            _const_spec(w_ssm_out.shape), _const_spec(w_out.shape),
        ],
        out_specs=_token_spec(TM_MIX, d),
        out_shape=jax.ShapeDtypeStruct((bsz, seq, d), F32),
        scratch_shapes=[pltpu.VMEM((TM_MIX + CONV_HALO, CONV_WIDTH), F32),
                        pltpu.VMEM((SSM_WIDTH // LANES, TM_MIX, LANES), F32)],
        compiler_params=pltpu.CompilerParams(
            dimension_semantics=("arbitrary", "arbitrary"), vmem_limit_bytes=VMEM_LIMIT),
        name="mixer",
    )(x1, u, yc, mod3, g, w_in, conv_w, w_conv_out, d_skip, w_glu, w_ssm_out, w_out)


def kernel(x, c, w_ada, b_ada, g_ffn1, w1_a, w3_a, w2_a, g_mix, w_in, conv_w, w_conv_out, a_re, a_im,
           b_re, b_im, c_re, c_im, log_dt, d_skip, w_glu, w_ssm_out, w_out, g_ffn2, w1_b, w3_b, w2_b,
           g_final):
    assert w_ada.shape[0] == 1, "the final norm is fused into the last FFN: single-layer stacks only"
    bsz = x.shape[0]
    bf = lambda w: w[0].astype(BF16)
    mod3 = _mod_call(c, w_ada[0], b_ada[0]).reshape(bsz, N_MOD, D_MODEL)
    w_in_l = bf(w_in)
    x1, u, uc = _ffn_a_call(
        x, mod3, g_ffn1, g_mix, bf(w1_a), bf(w3_a), bf(w2_a), w_in_l[:, _U:_U + SSM_WIDTH])
    ops = _ssm_operators(a_re[0], a_im[0], b_re[0], b_im[0], c_re[0], c_im[0], log_dt[0])
    yc = _ssm_call(uc, ops)
    x2 = _mixer_call(x1, u, yc, mod3, g_mix, w_in_l, conv_w[0], bf(w_conv_out), d_skip,
                     bf(w_glu), bf(w_ssm_out), bf(w_out))
    return _ffn_b_call(x2, mod3, g_ffn2, g_final[None], bf(w1_b), bf(w3_b), bf(w2_b))
```

```python
import functools

import jax
import jax.numpy as jnp
from jax import lax
from jax.experimental import pallas as pl
from jax.experimental.pallas import tpu as pltpu

D_MODEL = 1024
D_FF = 2816
CONV_WIDTH = 1024
CONV_K = 3
SSM_WIDTH = 512
SSM_GROUP = 16
SSM_GROUPS = SSM_WIDTH // SSM_GROUP
SSM_STATE = 64
N_MOD = 9
EPS = 1e-6

SSM_CHUNK = 16
SSM_ROW = SSM_CHUNK * SSM_GROUP
SSM_PAIR_STATE = 2 * SSM_STATE
LANES = 128
SUBLANES = 8
BLOCKS_PER_VREG = LANES // SSM_GROUP
CONV_HALO = 8

V7X_VMEM_BYTES = 64 * 1024 * 1024
VMEM_LIMIT = V7X_VMEM_BYTES - 8 * 1024 * 1024

TM_FFN = 512
TM_MIX = 256
TN_MOD = 1024

BF16 = jnp.bfloat16
F32 = jnp.float32


def _dot(a, b):
    return jnp.dot(a, b, preferred_element_type=F32)


def _const_spec(shape):
    zeros = (0,) * len(shape)
    return pl.BlockSpec(shape, lambda *_: zeros, pipeline_mode=pl.Buffered(1))


def _norm_mod(x, g, shift, scale):
    y = x * lax.rsqrt(jnp.mean(x * x, axis=-1, keepdims=True) + EPS)
    return (y * g) * (1.0 + scale) + shift


def _split_bf16(v):
    hi = v.astype(BF16)
    lo = (v - hi.astype(F32)).astype(BF16)
    return hi, lo


def _mod_kernel(c_ref, w_ref, b_ref, o_ref):
    c = c_ref[...]
    cond = c * jax.nn.sigmoid(c)
    c_hi, c_lo = _split_bf16(cond)
    w_hi, w_lo = _split_bf16(w_ref[...])
    acc = _dot(c_hi, w_hi) + (_dot(c_hi, w_lo) + _dot(c_lo, w_hi))
    o_ref[...] = acc + b_ref[...]


def _mod_call(c, w_ada, b_ada):
    bsz, d = c.shape
    n = w_ada.shape[1]
    return pl.pallas_call(
        _mod_kernel,
        grid=(n // TN_MOD,),
        in_specs=[
            pl.BlockSpec((bsz, d), lambda j: (0, 0)),
            pl.BlockSpec((d, TN_MOD), lambda j: (0, j)),
            pl.BlockSpec((1, TN_MOD), lambda j: (0, j)),
        ],
        out_specs=pl.BlockSpec((bsz, TN_MOD), lambda j: (0, j)),
        out_shape=jax.ShapeDtypeStruct((bsz, n), F32),
        compiler_params=pltpu.CompilerParams(dimension_semantics=("arbitrary",)),
        name="mod",
    )(c, w_ada, b_ada.reshape(1, n))


def _swiglu_update(x, m, row, g, w1_ref, w3_ref, w2_ref):
    h = _norm_mod(x, g, m[row:row + 1], m[row + 1:row + 2]).astype(BF16)
    a = _dot(h, w1_ref[...])
    b = _dot(h, w3_ref[...])
    act = ((a * jax.nn.sigmoid(a)) * b).astype(BF16)
    y = _dot(act, w2_ref[...])
    return x + (0.5 * m[row + 2:row + 3]) * y


def _block_bit_masks(rows):
    blk = lax.broadcasted_iota(jnp.int32, (rows, LANES), 1) // SSM_GROUP
    return {d: (blk & d) != 0 for d in (4, 2, 1)}


def _block_transpose(vs, masks):
    for d in (4, 2, 1):
        nxt = list(vs)
        for s in range(BLOCKS_PER_VREG):
            if s & d == 0:
                lo, hi = vs[s], vs[s + d]
                nxt[s] = jnp.where(masks[d], pltpu.roll(hi, d * SSM_GROUP, 1), lo)
                nxt[s + d] = jnp.where(masks[d], hi, pltpu.roll(lo, LANES - d * SSM_GROUP, 1))
        vs = nxt
    return vs


def _ffn_a_kernel(x_ref, mod_ref, g1_ref, g2_ref, w1_ref, w3_ref, w2_ref, wu_ref,
                  x1_ref, u_ref, uc_ref, slab_ref):
    m = mod_ref[0]
    x1 = _swiglu_update(x_ref[0], m, 0, g1_ref[...], w1_ref, w3_ref, w2_ref)
    x1_ref[0] = x1
    h2 = _norm_mod(x1, g2_ref[...], m[3:4], m[4:5]).astype(BF16)
    u = _dot(h2, wu_ref[...])
    u_ref[0] = u
    n_rows = u.shape[0] // SSM_CHUNK
    masks = _block_bit_masks(n_rows)
    for v in range(SSM_WIDTH // LANES):
        slab_ref[v] = u[:, v * LANES:(v + 1) * LANES]
    for v in range(SSM_WIDTH // LANES):
        for j in range(SSM_ROW // LANES):
            toks = [slab_ref[v, pl.ds(BLOCKS_PER_VREG * j + sb, n_rows, stride=SSM_CHUNK), :]
                    for sb in range(BLOCKS_PER_VREG)]
            for gb, row in enumerate(_block_transpose(toks, masks)):
                uc_ref[BLOCKS_PER_VREG * v + gb, 0, :, j * LANES:(j + 1) * LANES] = row.astype(BF16)


def _ffn_b_kernel(x_ref, mod_ref, g_ref, gf_ref, w1_ref, w3_ref, w2_ref, o_ref):
    x3 = _swiglu_update(x_ref[0], mod_ref[0], 6, g_ref[...], w1_ref, w3_ref, w2_ref)
    y = x3 * lax.rsqrt(jnp.mean(x3 * x3, axis=-1, keepdims=True) + EPS)
    o_ref[0] = y * gf_ref[...]


def _token_spec(tm, width):
    return pl.BlockSpec((1, tm, width), lambda b, j: (b, j, 0))


def _mod_spec():
    return pl.BlockSpec((1, N_MOD, D_MODEL), lambda b, j: (b, 0, 0))


def _chunk_spec(tm):
    return pl.BlockSpec((SSM_GROUPS, 1, tm // SSM_CHUNK, SSM_ROW), lambda b, j: (0, b, j, 0))


def _ffn_a_call(x, mod3, g1, g2, w1, w3, w2, wu):
    bsz, seq, d = x.shape
    return pl.pallas_call(
        _ffn_a_kernel,
        grid=(bsz, seq // TM_FFN),
        in_specs=[
            _token_spec(TM_FFN, d), _mod_spec(), _const_spec((1, d)), _const_spec((1, d)),
            _const_spec(w1.shape), _const_spec(w3.shape), _const_spec(w2.shape), _const_spec(wu.shape),
        ],
        out_specs=[_token_spec(TM_FFN, d), _token_spec(TM_FFN, SSM_WIDTH), _chunk_spec(TM_FFN)],
        out_shape=[
            jax.ShapeDtypeStruct((bsz, seq, d), F32),
            jax.ShapeDtypeStruct((bsz, seq, SSM_WIDTH), F32),
            jax.ShapeDtypeStruct((SSM_GROUPS, bsz, seq // SSM_CHUNK, SSM_ROW), BF16),
        ],
        scratch_shapes=[pltpu.VMEM((SSM_WIDTH // LANES, TM_FFN, LANES), F32)],
        compiler_params=pltpu.CompilerParams(
            dimension_semantics=("arbitrary", "arbitrary"), vmem_limit_bytes=VMEM_LIMIT),
        name="ffn_a",
    )(x, mod3, g1, g2, w1, w3, w2, wu)


def _ffn_b_call(x, mod3, g, gf, w1, w3, w2):
    bsz, seq, d = x.shape
    return pl.pallas_call(
        _ffn_b_kernel,
        grid=(bsz, seq // TM_FFN),
        in_specs=[
            _token_spec(TM_FFN, d), _mod_spec(), _const_spec((1, d)), _const_spec((1, d)),
            _const_spec(w1.shape), _const_spec(w3.shape), _const_spec(w2.shape),
        ],
        out_specs=_token_spec(TM_FFN, d),
        out_shape=jax.ShapeDtypeStruct((bsz, seq, d), F32),
        compiler_params=pltpu.CompilerParams(
            dimension_semantics=("arbitrary", "arbitrary"), vmem_limit_bytes=VMEM_LIMIT),
        name="ffn_b",
    )(x, mod3, g, gf, w1, w3, w2)


def _ssm_operators(a_re, a_im, b_re, b_im, c_re, c_im, log_dt):
    g_n, p_n, t_n = SSM_GROUPS, SSM_STATE, SSM_CHUNK
    hp = lax.Precision.HIGHEST
    dt = jnp.exp(log_dt)[:, None]
    lr, li = (a_re * dt)[..., None], (a_im * dt)[..., None]
    tau = jnp.arange(t_n + 1, dtype=F32)
    mag = jnp.exp(lr * tau)
    pw_re, pw_im = mag * jnp.cos(li * tau), mag * jnp.sin(li * tau)
    nr, ni = pw_re[..., 1] - 1.0, pw_im[..., 1]
    den = a_re * a_re + a_im * a_im
    q_re, q_im = (nr * a_re + ni * a_im) / den, (ni * a_re - nr * a_im) / den
    bb_re = q_re[..., None] * b_re - q_im[..., None] * b_im
    bb_im = q_re[..., None] * b_im + q_im[..., None] * b_re
    ct_re, ct_im = jnp.swapaxes(c_re, 1, 2)[:, :, None, :], jnp.swapaxes(c_im, 1, 2)[:, :, None, :]
    wide = (g_n, p_n, (t_n + 1) * SSM_GROUP)
    cp_re = (pw_re[..., None] * ct_re - pw_im[..., None] * ct_im).reshape(wide)
    cp_im = (pw_re[..., None] * ct_im + pw_im[..., None] * ct_re).reshape(wide)
    k_lag = (jnp.einsum("gpi,gpx->gix", bb_re, cp_re[..., :SSM_ROW], precision=hp)
             - jnp.einsum("gpi,gpx->gix", bb_im, cp_im[..., :SSM_ROW], precision=hp))
    k_pad = jnp.concatenate([jnp.zeros_like(k_lag), k_lag], axis=-1)
    m_op = jnp.stack([k_pad[..., SSM_ROW - SSM_GROUP * s:2 * SSM_ROW - SSM_GROUP * s]
                      for s in range(t_n)], axis=1).reshape(g_n, SSM_ROW, SSM_ROW)
    rev_re, rev_im = pw_re[..., t_n - 1::-1][..., None], pw_im[..., t_n - 1::-1][..., None]
    bb4 = bb_re[:, :, None, :], bb_im[:, :, None, :]
    w_re = (rev_re * bb4[0] - rev_im * bb4[1]).reshape(g_n, p_n, SSM_ROW)
    w_im = (rev_re * bb4[1] + rev_im * bb4[0]).reshape(g_n, p_n, SSM_ROW)
    w_re = jnp.swapaxes(w_re, 1, 2).reshape(g_n // 2, 2, SSM_ROW, p_n)
    w_im = jnp.swapaxes(w_im, 1, 2).reshape(g_n // 2, 2, SSM_ROW, p_n)
    z = jnp.zeros_like(w_re[:, 0])
    w_op = jnp.concatenate([
        jnp.concatenate([w_re[:, 0], z, w_im[:, 0], z], axis=-1),
        jnp.concatenate([z, w_re[:, 1], z, w_im[:, 1]], axis=-1)], axis=1)
    o_re = cp_re[..., SSM_GROUP:].reshape(g_n // 2, 2, p_n, SSM_ROW)
    o_im = (-cp_im[..., SSM_GROUP:]).reshape(g_n // 2, 2, p_n, SSM_ROW)
    zo = jnp.zeros_like(o_re[:, 0])
    o_op = jnp.concatenate([
        jnp.concatenate([o_re[:, 0], zo], axis=-1), jnp.concatenate([zo, o_re[:, 1]], axis=-1),
        jnp.concatenate([o_im[:, 0], zo], axis=-1), jnp.concatenate([zo, o_im[:, 1]], axis=-1)],
        axis=1)
    a_op = jnp.stack([pw_re[..., t_n].reshape(g_n // 2, SSM_PAIR_STATE),
                      pw_im[..., t_n].reshape(g_n // 2, SSM_PAIR_STATE)], axis=1)
    return m_op.astype(BF16), w_op.astype(BF16), o_op.astype(BF16), a_op


def _scan_pitch(n_chunks):
    tiles = n_chunks // SUBLANES
    return (tiles + 1 - tiles % 2) * SUBLANES


def _ssm_kernel(u_ref, m_ref, w_ref, o_ref, a_ref, y_ref, wu_ref, hs_ref, *, bsz, n_chunks):
    ps = SSM_PAIR_STATE
    pitch = _scan_pitch(n_chunks)
    u0 = u_ref[0]
    u1 = u_ref[1]
    wu = _dot(u0, w_ref[0, 0:SSM_ROW, :]) + _dot(u1, w_ref[0, SSM_ROW:2 * SSM_ROW, :])
    for b in range(bsz):
        wu_ref[0, b * pitch:b * pitch + n_chunks, :] = wu[b * n_chunks:(b + 1) * n_chunks, 0:ps]
        wu_ref[1, b * pitch:b * pitch + n_chunks, :] = wu[b * n_chunks:(b + 1) * n_chunks, ps:2 * ps]
    a_re = jnp.broadcast_to(a_ref[0, 0:1, :], (bsz, ps))
    a_im = jnp.broadcast_to(a_ref[0, 1:2, :], (bsz, ps))

    def step(k, carry):
        h_re, h_im = carry
        rows_k = pl.ds(k, bsz, stride=pitch)
        hs_ref[0, rows_k, :] = h_re
        hs_ref[1, rows_k, :] = h_im
        n_re = (a_re * h_re - a_im * h_im) + wu_ref[0, rows_k, :]
        n_im = (a_re * h_im + a_im * h_re) + wu_ref[1, rows_k, :]
        return n_re, n_im

    zero = jnp.zeros((bsz, ps), F32)
    lax.fori_loop(0, n_chunks, step, (zero, zero))
    hs = jnp.concatenate(
        [jnp.concatenate([hs_ref[0, b * pitch:b * pitch + n_chunks, :],
                          hs_ref[1, b * pitch:b * pitch + n_chunks, :]], axis=1) for b in range(bsz)],
        axis=0).astype(BF16)
    y_ref[0] = _dot(u0, m_ref[0]) + _dot(hs, o_ref[0, :, 0:SSM_ROW])
    y_ref[1] = _dot(u1, m_ref[1]) + _dot(hs, o_ref[0, :, SSM_ROW:2 * SSM_ROW])


def _ssm_call(uc, ops):
    g_n, bsz, n_chunks, _ = uc.shape
    rows = n_chunks * bsz
    m_op, w_op, o_op, a_op = ops
    yc = pl.pallas_call(
        functools.partial(_ssm_kernel, bsz=bsz, n_chunks=n_chunks),
        grid=(g_n // 2,),
        in_specs=[
            pl.BlockSpec((2, rows, SSM_ROW), lambda p: (p, 0, 0)),
            pl.BlockSpec((2, SSM_ROW, SSM_ROW), lambda p: (p, 0, 0)),
            pl.BlockSpec((1, 2 * SSM_ROW, 2 * SSM_PAIR_STATE), lambda p: (p, 0, 0)),
            pl.BlockSpec((1, 2 * SSM_PAIR_STATE, 2 * SSM_ROW), lambda p: (p, 0, 0)),
            pl.BlockSpec((1, 2, SSM_PAIR_STATE), lambda p: (p, 0, 0)),
        ],
        out_specs=pl.BlockSpec((2, rows, SSM_ROW), lambda p: (p, 0, 0)),
        out_shape=jax.ShapeDtypeStruct((g_n, rows, SSM_ROW), F32),
        scratch_shapes=[
            pltpu.VMEM((2, bsz * _scan_pitch(n_chunks), SSM_PAIR_STATE), F32),
            pltpu.VMEM((2, bsz * _scan_pitch(n_chunks), SSM_PAIR_STATE), F32),
        ],
        compiler_params=pltpu.CompilerParams(
            dimension_semantics=("arbitrary",), vmem_limit_bytes=VMEM_LIMIT),
        name="ssm",
    )(uc.reshape(g_n, rows, SSM_ROW), m_op, w_op, o_op, a_op)
    return yc.reshape(g_n, bsz, n_chunks, SSM_ROW)


_GB, _GC, _V = 0, CONV_WIDTH, 2 * CONV_WIDTH
_U = 3 * CONV_WIDTH
_GA = _U + SSM_WIDTH
_GBB = _GA + D_MODEL


def _mixer_kernel(x_ref, u_ref, yc_ref, mod_ref, g_ref, win_ref, cw_ref, wco_ref, dsk_ref,
                  wglu_ref, wso_ref, wout_ref, o_ref, cv_ref, slab_ref):
    tm = x_ref.shape[1]
    hl = CONV_HALO
    m = mod_ref[0]
    x1 = x_ref[0]
    h2 = _norm_mod(x1, g_ref[...], m[3:4], m[4:5]).astype(BF16)

    @pl.when(pl.program_id(1) == 0)
    def _():
        cv_ref[0:hl, :] = jnp.zeros((hl, CONV_WIDTH), F32)

    cv = _dot(h2, win_ref[:, _GC:_GC + CONV_WIDTH]) * _dot(h2, win_ref[:, _V:_V + CONV_WIDTH])
    cv_ref[hl:hl + tm, :] = cv
    cw = cw_ref[...]
    conv = (cw[2:3] * cv + cw[1:2] * cv_ref[hl - 1:hl - 1 + tm, :]) + cw[0:1] * cv_ref[hl - 2:hl - 2 + tm, :]
    cv_ref[0:hl, :] = cv_ref[tm:tm + hl, :]
    gate_b = _dot(h2, win_ref[:, _GB:_GB + CONV_WIDTH])
    y_a = _dot((gate_b * conv).astype(BF16), wco_ref[...])

    n_rows = tm // SSM_CHUNK
    masks = _block_bit_masks(n_rows)
    for v in range(SSM_WIDTH // LANES):
        for j in range(SSM_ROW // LANES):
            grps = [yc_ref[BLOCKS_PER_VREG * v + gb, 0, :, j * LANES:(j + 1) * LANES]
                    for gb in range(BLOCKS_PER_VREG)]
            for tb, row in enumerate(_block_transpose(grps, masks)):
                slab_ref[v, pl.ds(BLOCKS_PER_VREG * j + tb, n_rows, stride=SSM_CHUNK), :] = row
    ys = jnp.concatenate([slab_ref[v] for v in range(SSM_WIDTH // LANES)], axis=1)

    s = jax.nn.gelu(ys + dsk_ref[...] * u_ref[0])
    s = s * jax.nn.sigmoid(_dot(s.astype(BF16), wglu_ref[...]))
    y_b = _dot(s.astype(BF16), wso_ref[...])

    glog_a = _dot(h2, win_ref[:, _GA:_GA + D_MODEL])
    glog_b = _dot(h2, win_ref[:, _GBB:_GBB + D_MODEL])
    merged = jax.nn.sigmoid(glog_a) * y_a + jax.nn.sigmoid(glog_b) * y_b
    o_ref[0] = x1 + m[5:6] * _dot(merged.astype(BF16), wout_ref[...])


def _mixer_call(x1, u, yc, mod3, g, w_in, conv_w, w_conv_out, d_skip, w_glu, w_ssm_out, w_out):
    bsz, seq, d = x1.shape
    return pl.pallas_call(
        _mixer_kernel,
        grid=(bsz, seq // TM_MIX),
        in_specs=[
            _token_spec(TM_MIX, d), _token_spec(TM_MIX, SSM_WIDTH), _chunk_spec(TM_MIX),
            _mod_spec(), _const_spec((1, d)), _const_spec(w_in.shape), _const_spec(conv_w.shape),
            _const_spec(w_conv_out.shape), _const_spec((1, SSM_WIDTH)), _const_spec(w_glu.shape),
            _const_spec(w_ssm_out.shape), _const_spec(w_out.shape),
        ],
        out_specs=_token_spec(TM_MIX, d),
        out_shape=jax.ShapeDtypeStruct((bsz, seq, d), F32),
        scratch_shapes=[pltpu.VMEM((TM_MIX + CONV_HALO, CONV_WIDTH), F32),
                        pltpu.VMEM((SSM_WIDTH // LANES, TM_MIX, LANES), F32)],
        compiler_params=pltpu.CompilerParams(
            dimension_semantics=("arbitrary", "arbitrary"), vmem_limit_bytes=VMEM_LIMIT),
        name="mixer",
    )(x1, u, yc, mod3, g, w_in, conv_w, w_conv_out, d_skip, w_glu, w_ssm_out, w_out)


def kernel(x, c, w_ada, b_ada, g_ffn1, w1_a, w3_a, w2_a, g_mix, w_in, conv_w, w_conv_out, a_re, a_im,
           b_re, b_im, c_re, c_im, log_dt, d_skip, w_glu, w_ssm_out, w_out, g_ffn2, w1_b, w3_b, w2_b,
           g_final):
    assert w_ada.shape[0] == 1, "the final norm is fused into the last FFN: single-layer stacks only"
    bsz = x.shape[0]
    bf = lambda w: w[0].astype(BF16)
    mod3 = _mod_call(c, w_ada[0], b_ada[0]).reshape(bsz, N_MOD, D_MODEL)
    w_in_l = bf(w_in)
    x1, u, uc = _ffn_a_call(
        x, mod3, g_ffn1, g_mix, bf(w1_a), bf(w3_a), bf(w2_a), w_in_l[:, _U:_U + SSM_WIDTH])
    ops = _ssm_operators(a_re[0], a_im[0], b_re[0], b_im[0], c_re[0], c_im[0], log_dt[0])
    yc = _ssm_call(uc, ops)
    x2 = _mixer_call(x1, u, yc, mod3, g_mix, w_in_l, conv_w[0], bf(w_conv_out), d_skip,
                     bf(w_glu), bf(w_ssm_out), bf(w_out))
    return _ffn_b_call(x2, mod3, g_ffn2, g_final[None], bf(w1_b), bf(w3_b), bf(w2_b))
```
